```python
import math
import jax, jax.numpy as jnp
from jax import lax
import numpy as np

D_MODEL = 1024
BATCH = 16
SEQ = 2048
DEPTH = 2
DEC_BATCH = 32
DEC_SEQ = 4
PAST_LEN = 16384
PAGE_SIZE = 128

HEAD_DIM = 64
SB_HEADS = 8
SB_WIDTH = SB_HEADS * HEAD_DIM
DIFF_HEADS = 4
DIFF_VDIM = 2 * HEAD_DIM
DIFF_WIDTH = DIFF_HEADS * DIFF_VDIM
MIX_WIDTH = SB_WIDTH + DIFF_WIDTH
SB_Q = 0
SB_K = SB_WIDTH
SB_V = 2 * SB_WIDTH
DF_Q = 3 * SB_WIDTH
DF_K = DF_Q + DIFF_WIDTH
DF_V = DF_K + DIFF_WIDTH
IN_WIDTH = DF_V + DIFF_WIDTH
ROT_DIM = HEAD_DIM // 4
ROPE_THETA = 500000.0
Q_BLOCK = 128
D_FF = 2816
N_EXPERTS = 8
TOP_K = 2
D_FF_EXPERT = 3584
N_DENSE = (DEPTH + 1) // 2
N_MOE = DEPTH // 2
DN_ALPHA = (2 * DEPTH) ** 0.25
DN_BETA = (8 * DEPTH) ** -0.25
LN_EPS = 1e-5
N_PAGES = PAST_LEN // PAGE_SIZE

kernel_name = "hybrid_stickbreak_diffattn_decoder_step"

F32 = jnp.float32


def _layernorm(x, g, b):
    xf = x.astype(F32)
    mu = jnp.mean(xf, -1, keepdims=True)
    var = jnp.mean(jnp.square(xf - mu), -1, keepdims=True)
    return ((xf - mu) * lax.rsqrt(var + LN_EPS) * g.astype(F32) + b.astype(F32)).astype(x.dtype)


def _rope(x, pos):
    n_extra = x.ndim - 3
    freqs = ROPE_THETA ** (-jnp.arange(0, ROT_DIM, 2, dtype=F32) / ROT_DIM)
    ang = pos.astype(F32)[:, None] * freqs[None, :]
    ang = ang.reshape((1, ang.shape[0]) + (1,) * n_extra + (ROT_DIM // 2,))
    cos, sin = jnp.cos(ang), jnp.sin(ang)
    xf = x[..., :ROT_DIM].astype(F32)
    x1, x2 = xf[..., :ROT_DIM // 2], xf[..., ROT_DIM // 2:]
    rot = jnp.concatenate([x1 * cos - x2 * sin, x2 * cos + x1 * sin], -1).astype(x.dtype)
    return jnp.concatenate([rot, x[..., ROT_DIM:]], -1)


def _project(h, w_in, pos):
    b, t = h.shape[:2]
    z = jnp.matmul(h, w_in)
    sq = z[..., SB_Q:SB_K].reshape(b, t, SB_HEADS, HEAD_DIM)
    sk = z[..., SB_K:SB_V].reshape(b, t, SB_HEADS, HEAD_DIM)
    sv = z[..., SB_V:DF_Q].reshape(b, t, SB_HEADS, HEAD_DIM)
    dq = _rope(z[..., DF_Q:DF_K].reshape(b, t, DIFF_HEADS, 2, HEAD_DIM), pos)
    dk = _rope(z[..., DF_K:DF_V].reshape(b, t, DIFF_HEADS, 2, HEAD_DIM), pos)
    dv = z[..., DF_V:].reshape(b, t, DIFF_HEADS, DIFF_VDIM)
    return sq, sk, sv, dq, dk, dv


def _diff_lambda(lam_p, l):
    lp = lam_p.astype(F32)
    lam_init = 0.8 - 0.6 * math.exp(-0.3 * l)
    lam = jnp.exp(jnp.sum(lp[0] * lp[1])) - jnp.exp(jnp.sum(lp[2] * lp[3])) + lam_init
    return lam, lam_init


def _sb_weights(z, mask):
    sp = jnp.where(mask, jax.nn.softplus(z), 0.0)
    rest = lax.cumsum(sp, axis=z.ndim - 1, reverse=True) - sp
    return jnp.where(mask, jnp.exp(jax.nn.log_sigmoid(z) - rest), 0.0)


def _diff_combine(s, mask, lam):
    p = jax.nn.softmax(jnp.where(mask, s, -jnp.inf), axis=-1)
    return p[:, :, 0] - lam * p[:, :, 1]


def _diff_out(o, subln_g, lam_init):
    o = o * lax.rsqrt(jnp.mean(o * o, -1, keepdims=True) + LN_EPS) * subln_g.astype(F32)
    return o * (1.0 - lam_init)


def _mix_prompt(h, w_in, lam_p, subln_g, l):
    b, s_len = h.shape[:2]
    pos = jnp.arange(s_len)
    sq, sk, sv, dq, dk, dv = _project(h, w_in, pos)
    lam, lam_init = _diff_lambda(lam_p, l)
    scale = HEAD_DIM ** -0.5
    sb_blocks, df_blocks = [], []
    for start in range(0, s_len, Q_BLOCK):
        end = min(start + Q_BLOCK, s_len)
        qpos = jnp.arange(start, end)
        kpos = jnp.arange(end)
        z = jnp.einsum('bthd,bshd->bhts', sq[:, start:end], sk[:, :end], preferred_element_type=F32) * scale
        a = _sb_weights(z, kpos[None, :] < qpos[:, None])
        sb_blocks.append(jnp.einsum('bhts,bshd->bthd', a.astype(sv.dtype), sv[:, :end], preferred_element_type=F32))
        sc = jnp.einsum('bthcd,bshcd->bhcts', dq[:, start:end], dk[:, :end], preferred_element_type=F32) * scale
        w = _diff_combine(sc, kpos[None, :] <= qpos[:, None], lam)
        df_blocks.append(jnp.einsum('bhts,bshe->bthe', w.astype(dv.dtype), dv[:, :end], preferred_element_type=F32))
    sb = jnp.concatenate(sb_blocks, 1).reshape(b, s_len, SB_WIDTH)
    df = _diff_out(jnp.concatenate(df_blocks, 1), subln_g, lam_init).reshape(b, s_len, DIFF_WIDTH)
    mix = jnp.concatenate([sb, df], -1).astype(h.dtype)
    return mix, sk, sv, dk, dv


def _gather(cache, l, page_table):
    g = cache[l, page_table]
    return g.reshape((g.shape[0], g.shape[1] * g.shape[2]) + g.shape[3:])


def _mix_sample(h, w_in, lam_p, subln_g, l, cache_sb_k, cache_sb_v, cache_diff_k, cache_diff_v, page_table):
    b, t = h.shape[:2]
    pos = PAST_LEN + jnp.arange(t)
    sq, sk, sv, dq, dk, dv = _project(h, w_in, pos)
    lam, lam_init = _diff_lambda(lam_p, l)
    scale = HEAD_DIM ** -0.5
    ar = jnp.arange(t)
    past_ok = jnp.ones((t, PAST_LEN), dtype=bool)
    k_past = _gather(cache_sb_k, l, page_table)
    z = jnp.concatenate([
        jnp.einsum('bthd,bshd->bhts', sq, k_past.astype(sq.dtype), preferred_element_type=F32),
        jnp.einsum('bthd,bshd->bhts', sq, sk, preferred_element_type=F32)], -1) * scale
    a = _sb_weights(z, jnp.concatenate([past_ok, ar[None, :] < ar[:, None]], -1))
    v_past = _gather(cache_sb_v, l, page_table)
    o_sb = (jnp.einsum('bhts,bshd->bthd', a[..., :PAST_LEN].astype(sv.dtype), v_past.astype(sv.dtype), preferred_element_type=F32)
            + jnp.einsum('bhts,bshd->bthd', a[..., PAST_LEN:].astype(sv.dtype), sv, preferred_element_type=F32))
    dk_past = _gather(cache_diff_k, l, page_table)
    sc = jnp.concatenate([
        jnp.einsum('bthcd,bshcd->bhcts', dq, dk_past.astype(dq.dtype), preferred_element_type=F32),
        jnp.einsum('bthcd,bshcd->bhcts', dq, dk, preferred_element_type=F32)], -1) * scale
    w = _diff_combine(sc, jnp.concatenate([past_ok, ar[None, :] <= ar[:, None]], -1), lam)
    dv_past = _gather(cache_diff_v, l, page_table)
    o_df = (jnp.einsum('bhts,bshe->bthe', w[..., :PAST_LEN].astype(dv.dtype), dv_past.astype(dv.dtype), preferred_element_type=F32)
            + jnp.einsum('bhts,bshe->bthe', w[..., PAST_LEN:].astype(dv.dtype), dv, preferred_element_type=F32))
    sb = o_sb.reshape(b, t, SB_WIDTH)
    df = _diff_out(o_df, subln_g, lam_init).reshape(b, t, DIFF_WIDTH)
    mix = jnp.concatenate([sb, df], -1).astype(h.dtype)
    return mix, sk, sv, dk, dv


def _swiglu(x, wg, wu, wd):
    return jnp.matmul(jax.nn.silu(jnp.matmul(x, wg)) * jnp.matmul(x, wu), wd)


def _moe(x, w_router, wg, wu, wd):
    shp = x.shape
    xt = x.reshape(-1, shp[-1])
    logits = jnp.matmul(xt, w_router, preferred_element_type=F32)
    top_val, top_idx = lax.top_k(logits, TOP_K)
    gates = jax.nn.softmax(top_val, axis=-1)
    comb = jnp.sum(jax.nn.one_hot(top_idx, N_EXPERTS, dtype=F32) * gates[..., None], axis=1)
    y = jnp.zeros(xt.shape, F32)
    for e in range(N_EXPERTS):
        y = y + comb[:, e:e + 1] * _swiglu(xt, wg[e], wu[e], wd[e]).astype(F32)
    return y.astype(x.dtype).reshape(shp)


def _ffn(x, l, ffn_w_gate, ffn_w_up, ffn_w_down, moe_router, moe_w_gate, moe_w_up, moe_w_down):
    i = l // 2
    if l % 2 == 0:
        return _swiglu(x, ffn_w_gate[i], ffn_w_up[i], ffn_w_down[i])
    return _moe(x, moe_router[i], moe_w_gate[i], moe_w_up[i], moe_w_down[i])


def setup_inputs(seed: int = 0) -> dict:
    key = jax.random.key(seed)
    ks = jax.random.split(key, 24)
    n_used = DEC_BATCH * N_PAGES
    n_pool = n_used + max(1, n_used // 4)
    nrm = lambda k, shp, s: jax.random.normal(k, shp, F32) * s
    col_scale = jnp.ones((IN_WIDTH,), F32).at[SB_V:DF_Q].set(DN_BETA).at[DF_V:].set(DN_BETA)
    page_table = jax.random.permutation(ks[6], n_pool)[:n_used].reshape(DEC_BATCH, N_PAGES).astype(jnp.int32)
    return {
        "x_prompt": nrm(ks[0], (BATCH, SEQ, D_MODEL), 1.0),
        "x_sample": nrm(ks[1], (DEC_BATCH, DEC_SEQ, D_MODEL), 1.0),
        "cache_sb_k": nrm(ks[2], (DEPTH, n_pool, PAGE_SIZE, SB_HEADS, HEAD_DIM), 1.0),
        "cache_sb_v": nrm(ks[3], (DEPTH, n_pool, PAGE_SIZE, SB_HEADS, HEAD_DIM), DN_BETA),
        "cache_diff_k": nrm(ks[4], (DEPTH, n_pool, PAGE_SIZE, DIFF_HEADS, 2, HEAD_DIM), 1.0),
        "cache_diff_v": nrm(ks[5], (DEPTH, n_pool, PAGE_SIZE, DIFF_HEADS, DIFF_VDIM), DN_BETA),
        "page_table": page_table,
        "w_in": nrm(ks[7], (DEPTH, D_MODEL, IN_WIDTH), D_MODEL ** -0.5) * col_scale,
        "diff_lambda": nrm(ks[8], (DEPTH, 4, HEAD_DIM), 0.1),
        "diff_subln_g": 1.0 + nrm(ks[9], (DEPTH, DIFF_VDIM), 0.02),
        "w_out": nrm(ks[10], (DEPTH, MIX_WIDTH, D_MODEL), MIX_WIDTH ** -0.5 * DN_BETA),
        "ln1_g": 1.0 + nrm(ks[11], (DEPTH, D_MODEL), 0.02),
        "ln1_b": nrm(ks[12], (DEPTH, D_MODEL), 0.02),
        "ln2_g": 1.0 + nrm(ks[13], (DEPTH, D_MODEL), 0.02),
        "ln2_b": nrm(ks[14], (DEPTH, D_MODEL), 0.02),
        "ffn_w_gate": nrm(ks[15], (N_DENSE, D_MODEL, D_FF), D_MODEL ** -0.5),
        "ffn_w_up": nrm(ks[16], (N_DENSE, D_MODEL, D_FF), D_MODEL ** -0.5),
        "ffn_w_down": nrm(ks[17], (N_DENSE, D_FF, D_MODEL), D_FF ** -0.5 * DN_BETA),
        "moe_router": nrm(ks[18], (N_MOE, D_MODEL, N_EXPERTS), D_MODEL ** -0.5),
        "moe_w_gate": nrm(ks[19], (N_MOE, N_EXPERTS, D_MODEL, D_FF_EXPERT), D_MODEL ** -0.5),
        "moe_w_up": nrm(ks[20], (N_MOE, N_EXPERTS, D_MODEL, D_FF_EXPERT), D_MODEL ** -0.5),
        "moe_w_down": nrm(ks[21], (N_MOE, N_EXPERTS, D_FF_EXPERT, D_MODEL), D_FF_EXPERT ** -0.5 * DN_BETA),
    }


def reference(x_prompt, x_sample, cache_sb_k, cache_sb_v, cache_diff_k, cache_diff_v, page_table,
              w_in, diff_lambda, diff_subln_g, w_out, ln1_g, ln1_b, ln2_g, ln2_b,
              ffn_w_gate, ffn_w_up, ffn_w_down, moe_router, moe_w_gate, moe_w_up, moe_w_down):
    xp, xs = x_prompt, x_sample
    p_sk, p_sv, p_dk, p_dv = [], [], [], []
    s_sk, s_sv, s_dk, s_dv = [], [], [], []
    for l in range(DEPTH):
        mp, sk, sv, dk, dv = _mix_prompt(xp, w_in[l], diff_lambda[l], diff_subln_g[l], l)
        p_sk.append(sk); p_sv.append(sv); p_dk.append(dk); p_dv.append(dv)
        xp = _layernorm(DN_ALPHA * xp + jnp.matmul(mp, w_out[l]), ln1_g[l], ln1_b[l])
        ms, sk, sv, dk, dv = _mix_sample(xs, w_in[l], diff_lambda[l], diff_subln_g[l], l,
                                         cache_sb_k, cache_sb_v, cache_diff_k, cache_diff_v, page_table)
        s_sk.append(sk); s_sv.append(sv); s_dk.append(dk); s_dv.append(dv)
        xs = _layernorm(DN_ALPHA * xs + jnp.matmul(ms, w_out[l]), ln1_g[l], ln1_b[l])
        xp = _layernorm(DN_ALPHA * xp + _ffn(xp, l, ffn_w_gate, ffn_w_up, ffn_w_down, moe_router, moe_w_gate, moe_w_up, moe_w_down), ln2_g[l], ln2_b[l])
        xs = _layernorm(DN_ALPHA * xs + _ffn(xs, l, ffn_w_gate, ffn_w_up, ffn_w_down, moe_router, moe_w_gate, moe_w_up, moe_w_down), ln2_g[l], ln2_b[l])
    return (xp, xs,
            jnp.stack(p_sk, 0), jnp.stack(p_sv, 0), jnp.stack(p_dk, 0), jnp.stack(p_dv, 0),
            jnp.stack(s_sk, 0), jnp.stack(s_sv, 0), jnp.stack(s_dk, 0), jnp.stack(s_dv, 0))
```

```python
import functools
import math

import jax
import jax.numpy as jnp
from jax import lax
from jax.experimental import pallas as pl
from jax.experimental.pallas import tpu as pltpu

F32 = jnp.float32
BF16 = jnp.bfloat16

HEAD_DIM = 64
LANES = 128
SEG = 512
N_SEG = 6
ROT_DIM = HEAD_DIM // 4
ROPE_THETA = 500000.0
LN_EPS = 1e-5
PAGE_SIZE = 128
N_EXPERTS = 8
NEG_BIG = -1e30
VMEM_LIMIT = 56 * 1024 * 1024


def _params(*sem):
    return pltpu.CompilerParams(dimension_semantics=sem, vmem_limit_bytes=VMEM_LIMIT)


def _tile(n, pref):
    t = min(n, pref)
    assert n % t == 0, (n, t)
    return t


def _rope_tables(pos):
    half = ROT_DIM // 2
    freqs = ROPE_THETA ** (-jnp.arange(0, ROT_DIM, 2, dtype=F32) / ROT_DIM)
    ang = pos.astype(F32)[:, None] * freqs[None, :]
    cos, sin = jnp.cos(ang), jnp.sin(ang)
    j = jnp.arange(LANES) % HEAD_DIM
    first, second = j < half, (j >= half) & (j < ROT_DIM)
    idx = jnp.where(first, j, jnp.where(second, j - half, 0))
    cos_l, sin_l = cos[:, idx], sin[:, idx]
    c = jnp.where(first | second, cos_l, 1.0)
    sa = jnp.where(first, -sin_l, 0.0)
    sb = jnp.where(second, sin_l, 0.0)
    return c.astype(F32), sa.astype(F32), sb.astype(F32)


def _inproj_kernel(x_ref, w_ref, c_ref, sa_ref, sb_ref,
                   zb_ref, sk_ref, sv_ref, dk_ref, dv_ref):
    xb = x_ref[...].astype(BF16)
    scale = HEAD_DIM ** -0.5
    f32_outs = {1: sk_ref, 2: sv_ref, 4: dk_ref, 5: dv_ref}
    for seg in range(N_SEG):
        z = jnp.dot(xb, w_ref[:, seg * SEG:(seg + 1) * SEG], preferred_element_type=F32)
        if seg in (3, 4):
            c, sa, sb = c_ref[...], sa_ref[...], sb_ref[...]
            parts = []
            for g in range(SEG // LANES):
                zg = z[:, g * LANES:(g + 1) * LANES]
                parts.append(zg * c + pltpu.roll(zg, LANES - ROT_DIM // 2, 1) * sa
                             + pltpu.roll(zg, ROT_DIM // 2, 1) * sb)
            z = jnp.concatenate(parts, axis=1)
        if seg in f32_outs:
            f32_outs[seg][...] = z
        if seg in (0, 3):
            z = z * scale
        zb_ref[:, seg * SEG:(seg + 1) * SEG] = z.astype(BF16)


def _inproj(x, w_bf16, tables, tm_pref=512):
    m, d = x.shape
    c, sa, sb = tables
    p = c.shape[0]
    tm = _tile(math.gcd(m, p), tm_pref)
    nt = p // tm
    row = lambda i: (i, 0)
    tab = lambda i: (i % nt, 0)
    kv_shape = jax.ShapeDtypeStruct((m, SEG), F32)
    return pl.pallas_call(
        _inproj_kernel,
        grid=(m // tm,),
        in_specs=[pl.BlockSpec((tm, d), row),
                  pl.BlockSpec((d, N_SEG * SEG), lambda i: (0, 0)),
                  pl.BlockSpec((tm, LANES), tab),
                  pl.BlockSpec((tm, LANES), tab),
                  pl.BlockSpec((tm, LANES), tab)],
        out_specs=[pl.BlockSpec((tm, N_SEG * SEG), row)] + [pl.BlockSpec((tm, SEG), row)] * 4,
        out_shape=[jax.ShapeDtypeStruct((m, N_SEG * SEG), BF16)] + [kv_shape] * 4,
        compiler_params=_params("parallel"),
        name="inproj",
    )(x, w_bf16, c, sa, sb)


def _softplus(z):
    return jnp.maximum(z, 0.0) + jnp.log(1.0 + jnp.exp(-jnp.abs(z)))


def _nt_dot(a, b):
    return lax.dot_general(a, b, (((1,), (1,)), ((), ())), preferred_element_type=F32)


def _suffix_sum(sp, upper):
    hi = sp.astype(BF16)
    lo = (sp - hi.astype(F32)).astype(BF16)
    return (jnp.dot(hi, upper, preferred_element_type=F32)
            + jnp.dot(lo, upper, preferred_element_type=F32))


def _sb_tile(q, k, v, rest0, acc, upper, valid):
    z = _nt_dot(q, k)
    sp = _softplus(z)
    spm = sp if valid is None else jnp.where(valid, sp, 0.0)
    rest = _suffix_sum(spm, upper) + rest0
    a = jnp.exp(z - sp - rest)
    if valid is not None:
        a = jnp.where(valid, a, 0.0)
    acc = acc + jnp.dot(a.astype(BF16), v, preferred_element_type=F32)
    rest0 = rest0 + jnp.sum(spm, axis=-1, keepdims=True)
    return rest0, acc


def _softmax_tile(q, k, v, m, l, acc, valid):
    s = _nt_dot(q, k)
    if valid is not None:
        s = jnp.where(valid, s, NEG_BIG)
    m_new = jnp.maximum(m, jnp.max(s, axis=-1, keepdims=True))
    p = jnp.exp(s - m_new)
    alpha = jnp.exp(m - m_new)
    l = alpha * l + jnp.sum(p, axis=-1, keepdims=True)
    acc = alpha * acc + jnp.dot(p.astype(BF16), v, preferred_element_type=F32)
    return m_new, l, acc


def _diff_lambda(lam_ref, lam_init):
    lp = lam_ref[...]
    a = jnp.sum(lp[0:1, :] * lp[1:2, :], axis=-1, keepdims=True)
    b = jnp.sum(lp[2:3, :] * lp[3:4, :], axis=-1, keepdims=True)
    return jnp.exp(a) - jnp.exp(b) + lam_init


def _sub_rmsnorm(o, g, lam_init):
    o = o * lax.rsqrt(jnp.mean(o * o, axis=-1, keepdims=True) + LN_EPS) * g
    return o * (1.0 - lam_init)


def _sb_prompt_kernel(q_ref, k_ref, v_ref, o_ref, *, tq):
    i = pl.program_id(1)
    lane = lax.broadcasted_iota(jnp.int32, (tq, LANES), 1)
    low = lane < HEAD_DIM
    row = lax.broadcasted_iota(jnp.int32, (tq, tq), 0)
    col = lax.broadcasted_iota(jnp.int32, (tq, tq), 1)
    strict = col < row
    upper = jnp.where(row > col, 1.0, 0.0).astype(BF16)
    zeros_r = jnp.zeros((tq, 1), F32)
    zeros_a = jnp.zeros((tq, LANES), F32)
    for g in range(SEG // LANES):
        cols = slice(g * LANES, (g + 1) * LANES)
        qg = q_ref[:, cols].astype(F32)
        q0 = jnp.where(low, qg, 0.0).astype(BF16)
        q1 = jnp.where(low, 0.0, qg).astype(BF16)
        d0 = pl.multiple_of(i * tq, tq)
        kd, vd = k_ref[pl.ds(d0, tq), cols], v_ref[pl.ds(d0, tq), cols]
        r0, a0 = _sb_tile(q0, kd, vd, zeros_r, zeros_a, upper, strict)
        r1, a1 = _sb_tile(q1, kd, vd, zeros_r, zeros_a, upper, strict)

        def body(jj, carry, cols=cols, q0=q0, q1=q1):
            r0, a0, r1, a1 = carry
            s0 = pl.multiple_of((i - 1 - jj) * tq, tq)
            kj, vj = k_ref[pl.ds(s0, tq), cols], v_ref[pl.ds(s0, tq), cols]
            r0, a0 = _sb_tile(q0, kj, vj, r0, a0, upper, None)
            r1, a1 = _sb_tile(q1, kj, vj, r1, a1, upper, None)
            return r0, a0, r1, a1

        r0, a0, r1, a1 = lax.fori_loop(0, i, body, (r0, a0, r1, a1))
        o_ref[:, cols] = jnp.where(low, a0, a1).astype(o_ref.dtype)


def _diff_prompt_kernel(q_ref, k_ref, v_ref, lam_ref, g_ref, o_ref, *, tq, lam_init):
    i = pl.program_id(1)
    lane = lax.broadcasted_iota(jnp.int32, (tq, LANES), 1)
    low = lane < HEAD_DIM
    row = lax.broadcasted_iota(jnp.int32, (tq, tq), 0)
    col = lax.broadcasted_iota(jnp.int32, (tq, tq), 1)
    incl = col <= row
    lam = _diff_lambda(lam_ref, lam_init)
    m_init = jnp.full((tq, 1), NEG_BIG, F32)
    l_init = jnp.zeros((tq, 1), F32)
    a_init = jnp.zeros((tq, LANES), F32)
    for g in range(SEG // LANES):
        cols = slice(g * LANES, (g + 1) * LANES)
        qg = q_ref[:, cols].astype(F32)
        q0 = jnp.where(low, qg, 0.0).astype(BF16)
        q1 = jnp.where(low, 0.0, qg).astype(BF16)
        d0 = pl.multiple_of(i * tq, tq)
        kd, vd = k_ref[pl.ds(d0, tq), cols], v_ref[pl.ds(d0, tq), cols]
        c0 = _softmax_tile(q0, kd, vd, m_init, l_init, a_init, incl)
        c1 = _softmax_tile(q1, kd, vd, m_init, l_init, a_init, incl)

        def body(j, carry, cols=cols, q0=q0, q1=q1):
            s0 = pl.multiple_of(j * tq, tq)
            kj, vj = k_ref[pl.ds(s0, tq), cols], v_ref[pl.ds(s0, tq), cols]
            c0 = _softmax_tile(q0, kj, vj, *carry[:3], None)
            c1 = _softmax_tile(q1, kj, vj, *carry[3:], None)
            return c0 + c1

        m0, l0, a0, m1, l1, a1 = lax.fori_loop(0, i, body, c0 + c1)
        o = a0 / l0 - lam * (a1 / l1)
        o_ref[:, cols] = _sub_rmsnorm(o, g_ref[...], lam_init).astype(o_ref.dtype)


def _prompt_attention(zb, lam_p, subln_g, lam_init, tq=128):
    b, s, _ = zb.shape
    tq = _tile(s, tq)
    grid = (b, s // tq)
    qspec = lambda c: pl.BlockSpec((None, tq, SEG), lambda bi, i, c=c: (bi, i, c))
    kvspec = lambda c: pl.BlockSpec((None, s, SEG), lambda bi, i, c=c: (bi, 0, c))
    ospec = pl.BlockSpec((None, tq, SEG), lambda bi, i: (bi, i, 0))
    oshape = jax.ShapeDtypeStruct((b, s, SEG), BF16)
    sb = pl.pallas_call(
        functools.partial(_sb_prompt_kernel, tq=tq),
        grid=grid,
        in_specs=[qspec(0), kvspec(1), kvspec(2)],
        out_specs=ospec, out_shape=oshape,
        compiler_params=_params("parallel", "arbitrary"),
        name="sb_prompt",
    )(zb, zb, zb)
    df = pl.pallas_call(
        functools.partial(_diff_prompt_kernel, tq=tq, lam_init=lam_init),
        grid=grid,
        in_specs=[qspec(3), kvspec(4), kvspec(5),
                  pl.BlockSpec((4, HEAD_DIM), lambda bi, i: (0, 0)),
                  pl.BlockSpec((1, LANES), lambda bi, i: (0, 0))],
        out_specs=ospec, out_shape=oshape,
        compiler_params=_params("parallel", "arbitrary"),
        name="diff_prompt",
    )(zb, zb, zb, lam_p, subln_g.reshape(1, LANES))
    return sb, df


def _expand_queries(q, t_new):
    rows = jnp.concatenate(
        [jnp.broadcast_to(q[t:t + 1, :], (8, SEG)) for t in range(t_new)], axis=0)
    sub = lax.broadcasted_iota(jnp.int32, rows.shape, 0) % 8
    lane_sub = lax.broadcasted_iota(jnp.int32, rows.shape, 1) // HEAD_DIM
    return jnp.where(sub == lane_sub, rows, 0.0).astype(BF16)


def _sample_attn_kernel(pt_ref, zq_ref, nsk_ref, nsv_ref, ndk_ref, ndv_ref, lam_ref, g_ref,
                        *rest, t_new, n_pp, lam_init):
    del pt_ref
    caches = rest[:4 * n_pp]
    osb_ref, odf_ref = rest[4 * n_pp:4 * n_pp + 2]
    qs_ref, qd_ref, r_ref, asb_ref, m_ref, l_ref, adf_ref = rest[4 * n_pp + 2:]
    j = pl.program_id(1)
    nq = t_new * 8
    row = lax.broadcasted_iota(jnp.int32, (PAGE_SIZE, PAGE_SIZE), 0)
    col = lax.broadcasted_iota(jnp.int32, (PAGE_SIZE, PAGE_SIZE), 1)
    upper = jnp.where(row > col, 1.0, 0.0).astype(BF16)

    def sb_page(k, v, valid):
        r, a = _sb_tile(qs_ref[...], k.astype(BF16), v.astype(BF16),
                        r_ref[...], asb_ref[...], upper, valid)
        r_ref[...] = r
        asb_ref[...] = a

    def df_page(k, v, valid):
        m, l, a = _softmax_tile(qd_ref[...], k.astype(BF16), v.astype(BF16),
                                m_ref[...], l_ref[...], adf_ref[...], valid)
        m_ref[...] = m
        l_ref[...] = l
        adf_ref[...] = a

    @pl.when(j == 0)
    def _():
        zq = zq_ref[...].astype(F32)
        qs_ref[...] = _expand_queries(zq[:, 0:SEG], t_new)
        qd_ref[...] = _expand_queries(zq[:, 3 * SEG:4 * SEG], t_new)
        r_ref[...] = jnp.zeros_like(r_ref)
        asb_ref[...] = jnp.zeros_like(asb_ref)
        m_ref[...] = jnp.full_like(m_ref, NEG_BIG)
        l_ref[...] = jnp.zeros_like(l_ref)
        adf_ref[...] = jnp.zeros_like(adf_ref)
        t_of_row = lax.broadcasted_iota(jnp.int32, (nq, PAGE_SIZE), 0) // 8
        s_of_col = lax.broadcasted_iota(jnp.int32, (nq, PAGE_SIZE), 1)
        sb_page(nsk_ref[...], nsv_ref[...], s_of_col < t_of_row)
        df_page(ndk_ref[...], ndv_ref[...], s_of_col <= t_of_row)

    for p in range(n_pp):
        sk, sv, dk, dv = caches[4 * p:4 * p + 4]
        sb_page(sk[...], sv[...], None)
        df_page(dk[...], dv[...], None)

    @pl.when(j == pl.num_programs(1) - 1)
    def _():
        sub = lax.broadcasted_iota(jnp.int32, (nq, SEG), 0) % 8
        lane = lax.broadcasted_iota(jnp.int32, (nq, SEG), 1)
        own_sb = sub == lane // HEAD_DIM
        sb = jnp.where(own_sb, asb_ref[...], 0.0).reshape(t_new, 8, SEG).sum(axis=1)
        osb_ref[...] = sb.astype(osb_ref.dtype)
        lam = _diff_lambda(lam_ref, lam_init)
        wrow = jnp.where(sub % 2 == 0, 1.0, -lam) / l_ref[...]
        own_df = sub // 2 == lane // LANES
        df = jnp.where(own_df, adf_ref[...] * wrow, 0.0).reshape(t_new, 8, SEG).sum(axis=1)
        parts = [_sub_rmsnorm(df[:, h * LANES:(h + 1) * LANES], g_ref[...], lam_init)
                 for h in range(SEG // LANES)]
        odf_ref[...] = jnp.concatenate(parts, axis=1).astype(odf_ref.dtype)


def _sample_attention(zb, new_kv, caches, layer, page_table, lam_p, subln_g, lam_init, n_pp=4):
    bs, t_new, _ = zb.shape
    n_pages = page_table.shape[1]
    n_pp = _tile(n_pages, n_pp)
    steps = n_pages // n_pp
    nq = t_new * 8

    def cache_spec(p):
        def imap(b, j, pt, p=p):
            return (layer, pt[b, n_pages - 1 - (j * n_pp + p)], 0, 0)
        return pl.BlockSpec((None, None, PAGE_SIZE, SEG), imap)

    per_b = lambda shape: pl.BlockSpec((None,) + shape, lambda b, j, pt: (b, 0, 0))
    const = lambda shape: pl.BlockSpec(shape, lambda b, j, pt: (0, 0))
    in_specs = [per_b((t_new, N_SEG * SEG))] + [per_b((PAGE_SIZE, SEG))] * 4
    in_specs += [const((4, HEAD_DIM)), const((1, LANES))]
    cache_args = []
    for p in range(n_pp):
        for c in caches:
            in_specs.append(cache_spec(p))
            cache_args.append(c)
    oshape = jax.ShapeDtypeStruct((bs, t_new, SEG), BF16)
    grid_spec = pltpu.PrefetchScalarGridSpec(
        num_scalar_prefetch=1,
        grid=(bs, steps),
        in_specs=in_specs,
        out_specs=[per_b((t_new, SEG))] * 2,
        scratch_shapes=[pltpu.VMEM((nq, SEG), BF16), pltpu.VMEM((nq, SEG), BF16),
                        pltpu.VMEM((nq, 1), F32), pltpu.VMEM((nq, SEG), F32),
                        pltpu.VMEM((nq, 1), F32), pltpu.VMEM((nq, 1), F32),
                        pltpu.VMEM((nq, SEG), F32)],
    )
    return pl.pallas_call(
        functools.partial(_sample_attn_kernel, t_new=t_new, n_pp=n_pp, lam_init=lam_init),
        grid_spec=grid_spec,
        out_shape=[oshape, oshape],
        compiler_params=_params("parallel", "arbitrary"),
        name="sample_attn",
    )(page_table, zb, *new_kv, lam_p, subln_g.reshape(1, LANES), *cache_args)


def _layernorm(r, g, b):
    mu = jnp.mean(r, axis=-1, keepdims=True)
    d = r - mu
    var = jnp.mean(d * d, axis=-1, keepdims=True)
    return d * lax.rsqrt(var + LN_EPS) * g + b


def _outproj_ln_kernel(a1_ref, a2_ref, w_ref, x_ref, g_ref, b_ref, o_ref, *, alpha):
    y = (jnp.dot(a1_ref[...], w_ref[0:SEG, :], preferred_element_type=F32)
         + jnp.dot(a2_ref[...], w_ref[SEG:2 * SEG, :], preferred_element_type=F32))
    o_ref[...] = _layernorm(alpha * x_ref[...] + y, g_ref[...], b_ref[...])


def _outproj_ln(a1, a2, w_bf16, x, g, b, alpha, tm_pref=512):
    m, d = x.shape
    tm = _tile(m, tm_pref)
    row = lambda i: (i, 0)
    const = lambda i: (0, 0)
    return pl.pallas_call(
        functools.partial(_outproj_ln_kernel, alpha=alpha),
        grid=(m // tm,),
        in_specs=[pl.BlockSpec((tm, SEG), row), pl.BlockSpec((tm, SEG), row),
                  pl.BlockSpec((2 * SEG, d), const), pl.BlockSpec((tm, d), row),
                  pl.BlockSpec((1, d), const), pl.BlockSpec((1, d), const)],
        out_specs=pl.BlockSpec((tm, d), row),
        out_shape=jax.ShapeDtypeStruct((m, d), F32),
        compiler_params=_params("parallel"),
        name="outproj_ln",
    )(a1, a2, w_bf16, x, g.reshape(1, d), b.reshape(1, d))


def _swiglu_chunk(xb, wg, wu, wd):
    hg = jnp.dot(xb, wg, preferred_element_type=F32)
    hu = jnp.dot(xb, wu, preferred_element_type=F32)
    h = hg * jax.nn.sigmoid(hg) * hu
    return jnp.dot(h.astype(BF16), wd, preferred_element_type=F32)


def _ffn_ln_kernel(x_ref, wg_ref, wu_ref, wd_ref, g_ref, b_ref, o_ref, xb_ref, acc_ref, *, alpha):
    f = pl.program_id(1)

    @pl.when(f == 0)
    def _():
        xb_ref[...] = x_ref[...].astype(BF16)
        acc_ref[...] = jnp.zeros_like(acc_ref)

    acc_ref[...] += _swiglu_chunk(xb_ref[...], wg_ref[...], wu_ref[...], wd_ref[...])

    @pl.when(f == pl.num_programs(1) - 1)
    def _():
        o_ref[...] = _layernorm(alpha * x_ref[...] + acc_ref[...], g_ref[...], b_ref[...])


def _ffn_chunk(f):
    for tf in (512, 256, 128):
        if f % tf == 0:
            return tf
    raise ValueError(f)


def _ffn_ln(x, wg, wu, wd, g, b, alpha, tm_pref=1024):
    m, d = x.shape
    f = wg.shape[1]
    tm = _tile(m, tm_pref)
    tf = _ffn_chunk(f)
    row = lambda i, j: (i, 0)
    const = lambda i, j: (0, 0)
    return pl.pallas_call(
        functools.partial(_ffn_ln_kernel, alpha=alpha),
        grid=(m // tm, f // tf),
        in_specs=[pl.BlockSpec((tm, d), row),
                  pl.BlockSpec((d, tf), lambda i, j: (0, j)),
                  pl.BlockSpec((d, tf), lambda i, j: (0, j)),
                  pl.BlockSpec((tf, d), lambda i, j: (j, 0)),
                  pl.BlockSpec((1, d), const), pl.BlockSpec((1, d), const)],
        out_specs=pl.BlockSpec((tm, d), row),
        out_shape=jax.ShapeDtypeStruct((m, d), F32),
        scratch_shapes=[pltpu.VMEM((tm, d), BF16), pltpu.VMEM((tm, d), F32)],
        compiler_params=_params("parallel", "arbitrary"),
        name="ffn_ln",
    )(x, wg, wu, wd, g.reshape(1, d), b.reshape(1, d))


def _split_bf16(a):
    hi = a.astype(BF16)
    return hi, (a - hi.astype(F32)).astype(BF16)


def _router_top2(x, wr):
    xh, xl = _split_bf16(x)
    wh, wl = _split_bf16(wr)
    logits = (jnp.dot(xh, wh, preferred_element_type=F32)
              + jnp.dot(xh, wl, preferred_element_type=F32)
              + jnp.dot(xl, wh, preferred_element_type=F32))
    lane = lax.broadcasted_iota(jnp.int32, logits.shape, 1).astype(F32)
    logits = jnp.where(lane < N_EXPERTS, logits, NEG_BIG)
    v1 = jnp.max(logits, axis=-1, keepdims=True)
    i1 = jnp.min(jnp.where(logits == v1, lane, float(LANES)), axis=-1, keepdims=True)
    rest = jnp.where(lane == i1, NEG_BIG, logits)
    v2 = jnp.max(rest, axis=-1, keepdims=True)
    i2 = jnp.min(jnp.where(rest == v2, lane, float(LANES)), axis=-1, keepdims=True)
    e2 = jnp.exp(v2 - v1)
    g1 = 1.0 / (1.0 + e2)
    g2 = e2 * g1
    return jnp.where(lane == i1, g1, 0.0) + jnp.where(lane == i2, g2, 0.0)


def _moe_ln_kernel(x_ref, wr_ref, wg_ref, wu_ref, wd_ref, g_ref, b_ref, o_ref,
                   xb_ref, comb_ref, acc_ref, *, alpha):
    e, f = pl.program_id(1), pl.program_id(2)

    @pl.when((e == 0) & (f == 0))
    def _():
        x = x_ref[...]
        xb_ref[...] = x.astype(BF16)
        comb_ref[...] = _router_top2(x, wr_ref[...])
        acc_ref[...] = jnp.zeros_like(acc_ref)

    comb = comb_ref[...]
    lane = lax.broadcasted_iota(jnp.int32, comb.shape, 1)
    ce = jnp.sum(jnp.where(lane == e, comb, 0.0), axis=-1, keepdims=True)
    acc_ref[...] += ce * _swiglu_chunk(xb_ref[...], wg_ref[...], wu_ref[...], wd_ref[...])

    @pl.when((e == pl.num_programs(1) - 1) & (f == pl.num_programs(2) - 1))
    def _():
        o_ref[...] = _layernorm(alpha * x_ref[...] + acc_ref[...], g_ref[...], b_ref[...])


def _moe_ln(x, w_router, wg, wu, wd, g, b, alpha, tm_pref=512):
    m, d = x.shape
    n_e, _, f = wg.shape
    tm = _tile(m, tm_pref)
    tf = _ffn_chunk(f)
    wr = jnp.zeros((d, LANES), F32).at[:, :n_e].set(w_router)
    row = lambda i, e, j: (i, 0)
    const = lambda i, e, j: (0, 0)
    return pl.pallas_call(
        functools.partial(_moe_ln_kernel, alpha=alpha),
        grid=(m // tm, n_e, f // tf),
        in_specs=[pl.BlockSpec((tm, d), row),
                  pl.BlockSpec((d, LANES), const),
                  pl.BlockSpec((None, d, tf), lambda i, e, j: (e, 0, j)),
                  pl.BlockSpec((None, d, tf), lambda i, e, j: (e, 0, j)),
                  pl.BlockSpec((None, tf, d), lambda i, e, j: (e, j, 0)),
                  pl.BlockSpec((1, d), const), pl.BlockSpec((1, d), const)],
        out_specs=pl.BlockSpec((tm, d), row),
        out_shape=jax.ShapeDtypeStruct((m, d), F32),
        scratch_shapes=[pltpu.VMEM((tm, d), BF16), pltpu.VMEM((tm, LANES), F32),
                        pltpu.VMEM((tm, d), F32)],
        compiler_params=_params("parallel", "arbitrary", "arbitrary"),
        name="moe_ln",
    )(x, wr, wg, wu, wd, g.reshape(1, d), b.reshape(1, d))


def kernel(x_prompt, x_sample, cache_sb_k, cache_sb_v, cache_diff_k, cache_diff_v, page_table,
           w_in, diff_lambda, diff_subln_g, w_out, ln1_g, ln1_b, ln2_g, ln2_b,
           ffn_w_gate, ffn_w_up, ffn_w_down, moe_router, moe_w_gate, moe_w_up, moe_w_down):
    depth = w_in.shape[0]
    b, s, d = x_prompt.shape
    bs, t_new, _ = x_sample.shape
    n_pages = page_table.shape[1]
    past_len = n_pages * PAGE_SIZE
    alpha = (2 * depth) ** 0.25
    pool = cache_sb_k.shape[1]
    caches = [c.reshape(depth, pool, PAGE_SIZE, SEG)
              for c in (cache_sb_k, cache_sb_v, cache_diff_k, cache_diff_v)]

    tab_p = _rope_tables(jnp.arange(s))
    tab_s = _rope_tables(past_len + jnp.tile(jnp.arange(t_new), bs))

    xp = x_prompt.reshape(b * s, d)
    xs = x_sample.reshape(bs * t_new, d)
    p_kv, s_kv = [], []
    for l in range(depth):
        lam_init = 0.8 - 0.6 * math.exp(-0.3 * l)
        w_in_l = w_in[l].astype(BF16)
        w_out_l = w_out[l].astype(BF16)

        zb, *kv = _inproj(xp, w_in_l, tab_p)
        p_kv.append(kv)
        sb, df = _prompt_attention(zb.reshape(b, s, N_SEG * SEG), diff_lambda[l],
                                   diff_subln_g[l], lam_init)
        xp = _outproj_ln(sb.reshape(b * s, SEG), df.reshape(b * s, SEG), w_out_l, xp,
                         ln1_g[l], ln1_b[l], alpha)

        zb, *kv = _inproj(xs, w_in_l, tab_s)
        s_kv.append(kv)
        new_kv = [jnp.pad(a.reshape(bs, t_new, SEG), ((0, 0), (0, PAGE_SIZE - t_new), (0, 0)))
                  for a in kv]
        sb, df = _sample_attention(zb.reshape(bs, t_new, N_SEG * SEG), new_kv, caches, l,
                                   page_table, diff_lambda[l], diff_subln_g[l], lam_init)
        xs = _outproj_ln(sb.reshape(bs * t_new, SEG), df.reshape(bs * t_new, SEG), w_out_l, xs,
                         ln1_g[l], ln1_b[l], alpha)

        i = l // 2
        if l % 2 == 0:
            ws = [w[i].astype(BF16) for w in (ffn_w_gate, ffn_w_up, ffn_w_down)]
            xp = _ffn_ln(xp, *ws, ln2_g[l], ln2_b[l], alpha)
            xs = _ffn_ln(xs, *ws, ln2_g[l], ln2_b[l], alpha)
        else:
            ws = [w[i].astype(BF16) for w in (moe_w_gate, moe_w_up, moe_w_down)]
            xp = _moe_ln(xp, moe_router[i], *ws, ln2_g[l], ln2_b[l], alpha)
            xs = _moe_ln(xs, moe_router[i], *ws, ln2_g[l], ln2_b[l], alpha)

    def stack(kvs, idx, lead, tail):
        return jnp.stack([kv[idx] for kv in kvs], 0).reshape((depth,) + lead + tail)

    hd = (SEG // HEAD_DIM, HEAD_DIM)
    dk_shape = (SEG // LANES, 2, HEAD_DIM)
    dv_shape = (SEG // LANES, LANES)
    return (xp.reshape(b, s, d), xs.reshape(bs, t_new, d),
            stack(p_kv, 0, (b, s), hd), stack(p_kv, 1, (b, s), hd),
            stack(p_kv, 2, (b, s), dk_shape), stack(p_kv, 3, (b, s), dv_shape),
            stack(s_kv, 0, (bs, t_new), hd), stack(s_kv, 1, (bs, t_new), hd),
            stack(s_kv, 2, (bs, t_new), dk_shape), stack(s_kv, 3, (bs, t_new), dv_shape))
```
